```python
import jax
import jax.numpy as jnp
from jax import lax
import numpy as np

D_MODEL = 1024
BATCH = 4
SEQ = 4096
DEPTH = 2

GRID_W = 64
RMS_EPS = 1e-6

NA_HEADS = 8
NA_HEAD_DIM = 64
NA_WIDTH = NA_HEADS * NA_HEAD_DIM
NA_KR = 8
NA_KC = 16
NA_QC = 16
NA_KCB = 2 * NA_QC
NA_NCB = GRID_W // NA_QC

FN_GROUPS = 8
FN_GROUP_DIM = 64
FN_WIDTH = FN_GROUPS * FN_GROUP_DIM

EVEN_IN = 3 * NA_WIDTH + FN_WIDTH
EVEN_MIX = NA_WIDTH + FN_WIDTH

ML_HEADS = 8
ML_QK_DIM = 64
ML_V_DIM = 128
ML_QK_WIDTH = ML_HEADS * ML_QK_DIM
ML_WIDTH = ML_HEADS * ML_V_DIM
ML_CHUNK = 64
ODD_IN = 2 * ML_QK_WIDTH + 2 * ML_WIDTH + 4 * ML_HEADS

MOE_GROUPS = 4
MOE_EXPERTS_PER_GROUP = 8
MOE_EXPERTS = MOE_GROUPS * MOE_EXPERTS_PER_GROUP
MOE_TOP_K = 2
MOE_FF = 512
MOE_BLOCK = 256

N_EVEN = (DEPTH + 1) // 2
N_ODD = DEPTH // 2

kernel_name = 'hybrid_natten_fnet_mlstm_hmoe_encoder'


def rms_norm(x, gain):
    xf = x.astype(jnp.float32)
    y = xf * lax.rsqrt(jnp.mean(xf * xf, axis=-1, keepdims=True) + RMS_EPS)
    return (y * gain.astype(jnp.float32)).astype(x.dtype)


def neighbourhood_attention(q, k, v, rel_bias):
    B, S, H, dh = q.shape
    rows = S // GRID_W
    kr = min(NA_KR, rows)
    r = jnp.arange(rows)
    row_idx = jnp.clip(r - kr // 2, 0, rows - kr)[:, None] + jnp.arange(kr)[None, :]
    qcol = jnp.arange(GRID_W).reshape(NA_NCB, NA_QC)
    blk_start = jnp.clip(qcol[:, 0] - NA_KC // 2, 0, GRID_W - NA_KCB)
    col_idx = blk_start[:, None] + jnp.arange(NA_KCB)[None, :]
    win_start = jnp.clip(qcol - NA_KC // 2, 0, GRID_W - NA_KC)
    kcol = col_idx[:, None, :]
    col_valid = (kcol >= win_start[..., None]) & (kcol < win_start[..., None] + NA_KC)
    dr = row_idx - r[:, None] + (NA_KR - 1)
    dc = jnp.clip(kcol - qcol[..., None] + (NA_KC - 1), 0, 2 * NA_KC - 2)
    bias = rel_bias.astype(jnp.float32)[:, dr[:, None, None, :, None], dc[None, :, :, None, :]]

    def gather_block(t):
        tg = t.reshape(B, rows, GRID_W, H, dh)
        return jnp.take(jnp.take(tg, row_idx, axis=1), col_idx, axis=3)

    kg = gather_block(k)
    vg = gather_block(v)
    qg = q.reshape(B, rows, NA_NCB, NA_QC, H, dh)
    s = jnp.einsum('brjqhd,brajchd->bhrjqac', qg, kg).astype(jnp.float32) * (dh ** -0.5) + bias[None]
    s = jnp.where(col_valid[:, :, None, :], s, -jnp.inf)
    p = jax.nn.softmax(s.reshape(*s.shape[:-2], kr * NA_KCB), axis=-1).reshape(s.shape)
    o = jnp.einsum('bhrjqac,brajchd->brjqhd', p.astype(v.dtype), vg)
    return o.reshape(B, S, H * dh)


def fourier_mix(u):
    B, S, _ = u.shape
    uf = u.astype(jnp.float32).reshape(B, S, FN_GROUPS, FN_GROUP_DIM)
    z = jnp.fft.fft2(uf, axes=(1, 3), norm='ortho').real
    return z.reshape(B, S, FN_WIDTH).astype(u.dtype)


def even_mixer(xn, w_in, q_gain, k_gain, rel_bias, w_out):
    B, S, _ = xn.shape
    u = xn @ w_in
    q, k, v, ub = jnp.split(u, [NA_WIDTH, 2 * NA_WIDTH, 3 * NA_WIDTH], axis=-1)
    heads = lambda t: t.reshape(B, S, NA_HEADS, NA_HEAD_DIM)
    q = rms_norm(heads(q), q_gain)
    k = rms_norm(heads(k), k_gain)
    a = neighbourhood_attention(q, k, heads(v), rel_bias)
    b = fourier_mix(ub)
    return jnp.concatenate([a, b], axis=-1) @ w_out


def mlstm_chunkwise(q, k, v, log_i, log_f):
    B, H, S, dk = q.shape
    dv = v.shape[-1]
    L = ML_CHUNK
    nc = S // L
    q = q.reshape(B, H, nc, L, dk)
    k = k.reshape(B, H, nc, L, dk)
    v = v.reshape(B, H, nc, L, dv)
    log_i = log_i.reshape(B, H, nc, L)
    b = jnp.cumsum(log_f.reshape(B, H, nc, L), axis=-1)
    g = b[..., -1]
    w_log = g[..., None] - b + log_i
    m_loc = jnp.max(w_log, axis=-1)
    w = jnp.exp(w_log - m_loc[..., None])
    dC = jnp.einsum('bhcl,bhcld,bhcle->bhcde', w, k, v)
    dn = jnp.einsum('bhcl,bhcld->bhcd', w, k)

    def step(carry, xs):
        C, n, m = carry
        g_c, dC_c, dn_c, ml_c = xs
        m_new = jnp.maximum(g_c + m, ml_c)
        a = jnp.exp(g_c + m - m_new)
        c = jnp.exp(ml_c - m_new)
        C_new = a[..., None, None] * C + c[..., None, None] * dC_c
        n_new = a[..., None] * n + c[..., None] * dn_c
        return (C_new, n_new, m_new), (C, n, m)

    init = (jnp.zeros((B, H, dk, dv), jnp.float32), jnp.zeros((B, H, dk), jnp.float32),
            jnp.zeros((B, H), jnp.float32))
    xs = (jnp.moveaxis(g, 2, 0), jnp.moveaxis(dC, 2, 0), jnp.moveaxis(dn, 2, 0), jnp.moveaxis(m_loc, 2, 0))
    _, (C_prev, n_prev, m_prev) = lax.scan(step, init, xs)
    C_prev = jnp.moveaxis(C_prev, 0, 2)
    n_prev = jnp.moveaxis(n_prev, 0, 2)
    m_prev = jnp.moveaxis(m_prev, 0, 2)

    tril = jnp.tril(jnp.ones((L, L), dtype=bool))
    d_log = jnp.where(tril, b[..., :, None] - b[..., None, :] + log_i[..., None, :], -jnp.inf)
    inter_log = b + m_prev[..., None]
    m_out = jnp.maximum(inter_log, jnp.max(d_log, axis=-1))
    d_w = jnp.exp(d_log - m_out[..., None])
    inter_w = jnp.exp(inter_log - m_out)
    s = jnp.einsum('bhcld,bhcsd->bhcls', q, k) * d_w
    num = jnp.einsum('bhcls,bhcse->bhcle', s, v) + inter_w[..., None] * jnp.einsum('bhcld,bhcde->bhcle', q, C_prev)
    den = jnp.sum(s, axis=-1) + inter_w * jnp.einsum('bhcld,bhcd->bhcl', q, n_prev)
    h = num / jnp.maximum(jnp.abs(den), jnp.exp(-m_out))[..., None]
    return h.reshape(B, H, S, dv)


def odd_mixer(xn, w_in, gate_bias, h_gain, w_out):
    B, S, _ = xn.shape
    u = xn @ w_in
    q, k, v, o, gates = jnp.split(
        u, [ML_QK_WIDTH, 2 * ML_QK_WIDTH, 2 * ML_QK_WIDTH + ML_WIDTH, 2 * ML_QK_WIDTH + 2 * ML_WIDTH], axis=-1)
    to_heads = lambda t, d: t.reshape(B, S, ML_HEADS, d).transpose(0, 2, 1, 3).astype(jnp.float32)
    q = to_heads(q, ML_QK_DIM)
    k = to_heads(k, ML_QK_DIM) * (ML_QK_DIM ** -0.5)
    v = to_heads(v, ML_V_DIM)
    gates = (gates.astype(jnp.float32) + gate_bias.astype(jnp.float32)).reshape(B, S, 4, ML_HEADS)
    gates = gates.transpose(2, 0, 3, 1)
    i_fw, f_fw, i_bw, f_bw = gates[0], gates[1], gates[2], gates[3]
    h_fw = mlstm_chunkwise(q, k, v, i_fw, jax.nn.log_sigmoid(f_fw))
    rev = lambda t: jnp.flip(t, axis=2)
    h_bw = rev(mlstm_chunkwise(rev(q), rev(k), rev(v), rev(i_bw), jax.nn.log_sigmoid(rev(f_bw))))
    h = (h_fw + h_bw).transpose(0, 2, 1, 3)
    h = rms_norm(h, h_gain.reshape(ML_HEADS, ML_V_DIM)).reshape(B, S, ML_WIDTH)
    h = h * jax.nn.sigmoid(o.astype(jnp.float32))
    return h.astype(xn.dtype) @ w_out


def hier_route(x2, w_rg, b_rg, w_re, b_re):
    T = x2.shape[0]
    pg = jax.nn.softmax((x2 @ w_rg).astype(jnp.float32) + b_rg.astype(jnp.float32), axis=-1)
    p_top, g_sel = lax.top_k(pg, 1)
    le = ((x2 @ w_re).astype(jnp.float32) + b_re.astype(jnp.float32)).reshape(T, MOE_GROUPS, MOE_EXPERTS_PER_GROUP)
    le_sel = le[jnp.arange(T), g_sel[:, 0]]
    pe = jax.nn.softmax(le_sel, axis=-1)
    v2, i2 = lax.top_k(pe, MOE_TOP_K)
    gate = p_top * v2 / jnp.sum(v2, axis=-1, keepdims=True)
    eid = g_sel * MOE_EXPERTS_PER_GROUP + i2
    return eid.astype(jnp.int32), gate


def expert_mix(x2, eid, gate, w_gate, w_up, w_down):
    T, D = x2.shape
    A = T * MOE_TOP_K
    n_blocks = (A + MOE_EXPERTS * (MOE_BLOCK - 1) + MOE_BLOCK - 1) // MOE_BLOCK
    P = n_blocks * MOE_BLOCK
    flat_e = eid.reshape(-1)
    flat_tok = jnp.broadcast_to(jnp.arange(T, dtype=jnp.int32)[:, None], (T, MOE_TOP_K)).reshape(-1)
    flat_g = gate.reshape(-1)
    order = jnp.argsort(flat_e)
    se, stok, sg = flat_e[order], flat_tok[order], flat_g[order]
    counts = jax.ops.segment_sum(jnp.ones((A,), jnp.int32), flat_e, num_segments=MOE_EXPERTS)
    starts = jnp.cumsum(counts) - counts
    padded = (counts + MOE_BLOCK - 1) // MOE_BLOCK * MOE_BLOCK
    pend = jnp.cumsum(padded)
    pstarts = pend - padded
    dest = pstarts[se] + (jnp.arange(A, dtype=jnp.int32) - starts[se])
    row_tok = jnp.full((P,), T, jnp.int32).at[dest].set(stok)
    row_gate = jnp.zeros((P,), jnp.float32).at[dest].set(sg)
    block_start = jnp.arange(n_blocks, dtype=jnp.int32) * MOE_BLOCK
    block_e = jnp.minimum(jnp.searchsorted(pend, block_start, side='right'), MOE_EXPERTS - 1)
    x_pad = jnp.concatenate([x2, jnp.zeros((1, D), x2.dtype)], axis=0)

    def block_fn(args):
        rows, e = args
        xb = x_pad[rows]
        h = jax.nn.silu(xb @ w_gate[e]) * (xb @ w_up[e])
        return h @ w_down[e]

    y_rows = lax.map(block_fn, (row_tok.reshape(n_blocks, MOE_BLOCK), block_e))
    y = jnp.zeros((T + 1, D), jnp.float32).at[row_tok].add(
        y_rows.reshape(P, D).astype(jnp.float32) * row_gate[:, None])
    return y[:T].astype(x2.dtype)


def hier_moe(xn, w_rg, b_rg, w_re, b_re, w_gate, w_up, w_down):
    B, S, D = xn.shape
    x2 = xn.reshape(B * S, D)
    eid, gate = hier_route(x2, w_rg, b_rg, w_re, b_re)
    return expert_mix(x2, eid, gate, w_gate, w_up, w_down).reshape(B, S, D)


def setup_inputs(seed: int = 0) -> dict:
    key = jax.random.key(seed)
    ks = jax.random.split(key, 21)
    nrm = lambda k, shape, scale: scale * jax.random.normal(k, shape, jnp.float32)
    gate_base = jnp.asarray([0.0, 3.0, 0.0, 3.0], jnp.float32)[None, :, None]
    od_gate_bias = (gate_base + nrm(ks[9], (N_ODD, 4, ML_HEADS), 0.1)).reshape(N_ODD, 4 * ML_HEADS)
    return {
        'x': nrm(ks[0], (BATCH, SEQ, D_MODEL), 1.0),
        'ev_norm': 1.0 + nrm(ks[1], (N_EVEN, D_MODEL), 0.05),
        'ev_w_in': nrm(ks[2], (N_EVEN, D_MODEL, EVEN_IN), D_MODEL ** -0.5),
        'ev_q_gain': 1.0 + nrm(ks[3], (N_EVEN, NA_HEAD_DIM), 0.05),
        'ev_k_gain': 1.0 + nrm(ks[4], (N_EVEN, NA_HEAD_DIM), 0.05),
        'ev_rel_bias': nrm(ks[5], (N_EVEN, NA_HEADS, 2 * NA_KR - 1, 2 * NA_KC - 1), 0.1),
        'ev_w_out': nrm(ks[6], (N_EVEN, EVEN_MIX, D_MODEL), EVEN_MIX ** -0.5),
        'od_norm': 1.0 + nrm(ks[7], (N_ODD, D_MODEL), 0.05),
        'od_w_in': nrm(ks[8], (N_ODD, D_MODEL, ODD_IN), D_MODEL ** -0.5),
        'od_gate_bias': od_gate_bias,
        'od_h_gain': 1.0 + nrm(ks[10], (N_ODD, ML_WIDTH), 0.05),
        'od_w_out': nrm(ks[11], (N_ODD, ML_WIDTH, D_MODEL), ML_WIDTH ** -0.5),
        'moe_norm': 1.0 + nrm(ks[12], (DEPTH, D_MODEL), 0.05),
        'moe_w_rg': nrm(ks[13], (DEPTH, D_MODEL, MOE_GROUPS), D_MODEL ** -0.5),
        'moe_b_rg': nrm(ks[14], (DEPTH, MOE_GROUPS), 0.01),
        'moe_w_re': nrm(ks[15], (DEPTH, D_MODEL, MOE_EXPERTS), D_MODEL ** -0.5),
        'moe_b_re': nrm(ks[16], (DEPTH, MOE_EXPERTS), 0.01),
        'moe_w_gate': nrm(ks[17], (DEPTH, MOE_EXPERTS, D_MODEL, MOE_FF), D_MODEL ** -0.5),
        'moe_w_up': nrm(ks[18], (DEPTH, MOE_EXPERTS, D_MODEL, MOE_FF), D_MODEL ** -0.5),
        'moe_w_down': nrm(ks[19], (DEPTH, MOE_EXPERTS, MOE_FF, D_MODEL), MOE_FF ** -0.5),
    }


def reference(x, ev_norm, ev_w_in, ev_q_gain, ev_k_gain, ev_rel_bias, ev_w_out,
              od_norm, od_w_in, od_gate_bias, od_h_gain, od_w_out,
              moe_norm, moe_w_rg, moe_b_rg, moe_w_re, moe_b_re, moe_w_gate, moe_w_up, moe_w_down):
    for layer in range(DEPTH):
        i = layer // 2
        if layer % 2 == 0:
            x = x + even_mixer(rms_norm(x, ev_norm[i]), ev_w_in[i], ev_q_gain[i], ev_k_gain[i],
                               ev_rel_bias[i], ev_w_out[i])
        else:
            x = x + odd_mixer(rms_norm(x, od_norm[i]), od_w_in[i], od_gate_bias[i], od_h_gain[i], od_w_out[i])
        x = x + hier_moe(rms_norm(x, moe_norm[layer]), moe_w_rg[layer], moe_b_rg[layer], moe_w_re[layer],
                         moe_b_re[layer], moe_w_gate[layer], moe_w_up[layer], moe_w_down[layer])
    return x
```

```python
import functools

import numpy as np
import jax
import jax.numpy as jnp
from jax import lax
from jax.experimental import pallas as pl
from jax.experimental.pallas import tpu as pltpu

F32 = jnp.float32
BF16 = jnp.bfloat16

D_MODEL = 1024
GRID_W = 64
RMS_EPS = 1e-6

NA_HEADS = 8
NA_HEAD_DIM = 64
NA_WIDTH = NA_HEADS * NA_HEAD_DIM
NA_KR = 8
NA_KC = 16

FN_GROUPS = 8
FN_GROUP_DIM = 64
FN_WIDTH = FN_GROUPS * FN_GROUP_DIM

ML_HEADS = 8
ML_QK_DIM = 64
ML_V_DIM = 128
ML_QK_WIDTH = ML_HEADS * ML_QK_DIM
ML_WIDTH = ML_HEADS * ML_V_DIM
ML_CHUNK = 64

MOE_GROUPS = 4
MOE_EXPERTS_PER_GROUP = 8
MOE_EXPERTS = MOE_GROUPS * MOE_EXPERTS_PER_GROUP
MOE_TOP_K = 2
MOE_FF = 512
MOE_BLOCK = 256

LANES = 128
VMEM_LIMIT = 56 * 1024 * 1024


def _params(*sem):
    return pltpu.CompilerParams(dimension_semantics=sem, vmem_limit_bytes=VMEM_LIMIT)


def _split3(a):
    h1 = a.astype(BF16)
    r1 = a - h1.astype(F32)
    h2 = r1.astype(BF16)
    h3 = (r1 - h2.astype(F32)).astype(BF16)
    return h1, h2, h3


def _dot_exact_rhs(a, b_bf16):
    h1, h2, h3 = _split3(a)
    d = functools.partial(jnp.dot, preferred_element_type=F32)
    return d(h1, b_bf16) + d(h2, b_bf16) + d(h3, b_bf16)


def _norm_proj_kernel(*refs, has_main, has_aux, want_xn):
    it = iter(refs)
    x_ref, g_ref = next(it), next(it)
    w_ref = next(it) if has_main else None
    wa_ref = next(it) if has_aux else None
    ba_ref = next(it) if has_aux else None
    o_ref = next(it) if has_main else None
    aux_ref = next(it) if has_aux else None
    xno_ref = next(it) if want_xn else None
    xn_ref = next(it)

    @pl.when(pl.program_id(1) == 0)
    def _():
        x = x_ref[...]
        ms = jnp.mean(x * x, axis=-1, keepdims=True)
        xn = x * lax.rsqrt(ms + RMS_EPS) * g_ref[...]
        xn_ref[...] = xn.astype(BF16)
        if want_xn:
            xno_ref[...] = xn.astype(BF16)
        if has_aux:
            x1, x2, _ = _split3(xn)
            w1, w2, _ = _split3(wa_ref[...])
            d = functools.partial(jnp.dot, preferred_element_type=F32)
            aux_ref[...] = d(x1, w1) + d(x1, w2) + d(x2, w1) + ba_ref[...]

    if has_main:
        o_ref[...] = jnp.dot(xn_ref[...], w_ref[...],
                             preferred_element_type=F32).astype(o_ref.dtype)


def _norm_proj(x2, gain, w_main=None, w_aux=None, b_aux=None, want_xn=False, tm=1024, tn=1024):
    T, D = x2.shape
    has_main, has_aux = w_main is not None, w_aux is not None
    n_main = w_main.shape[1] if has_main else tn
    grid = (T // tm, n_main // tn)
    in_specs = [pl.BlockSpec((tm, D), lambda i, j: (i, 0)),
                pl.BlockSpec((1, D), lambda i, j: (0, 0))]
    args = [x2, gain.reshape(1, D).astype(F32)]
    out_shape, out_specs = [], []
    if has_main:
        in_specs.append(pl.BlockSpec((D, tn), lambda i, j: (0, j)))
        args.append(w_main)
        out_shape.append(jax.ShapeDtypeStruct((T, n_main), BF16))
        out_specs.append(pl.BlockSpec((tm, tn), lambda i, j: (i, j)))
    if has_aux:
        in_specs += [pl.BlockSpec((D, LANES), lambda i, j: (0, 0)),
                     pl.BlockSpec((1, LANES), lambda i, j: (0, 0))]
        args += [w_aux, b_aux]
        out_shape.append(jax.ShapeDtypeStruct((T, LANES), F32))
        out_specs.append(pl.BlockSpec((tm, LANES), lambda i, j: (i, 0)))
    if want_xn:
        out_shape.append(jax.ShapeDtypeStruct((T, D), BF16))
        out_specs.append(pl.BlockSpec((tm, D), lambda i, j: (i, 0)))
    return pl.pallas_call(
        functools.partial(_norm_proj_kernel, has_main=has_main, has_aux=has_aux, want_xn=want_xn),
        grid=grid, in_specs=in_specs, out_specs=out_specs, out_shape=out_shape,
        scratch_shapes=[pltpu.VMEM((tm, D), BF16)],
        compiler_params=_params("parallel", "arbitrary"),
        name="norm_proj",
    )(*args)


def _proj_res_kernel(*refs, n_lhs):
    lhs = refs[:n_lhs]
    ws = refs[n_lhs:2 * n_lhs]
    res_ref, o_ref = refs[2 * n_lhs], refs[2 * n_lhs + 1]
    acc = res_ref[...]
    for a, w in zip(lhs, ws):
        acc = acc + jnp.dot(a[...], w[...], preferred_element_type=F32)
    o_ref[...] = acc


def _proj_res(lhs_list, w_list, res, tm=1024):
    T, N = res.shape
    n = len(lhs_list)
    in_specs = [pl.BlockSpec((tm, a.shape[1]), lambda i: (i, 0)) for a in lhs_list]
    in_specs += [pl.BlockSpec(w.shape, lambda i: (0, 0)) for w in w_list]
    in_specs.append(pl.BlockSpec((tm, N), lambda i: (i, 0)))
    return pl.pallas_call(
        functools.partial(_proj_res_kernel, n_lhs=n),
        grid=(T // tm,), in_specs=in_specs,
        out_specs=pl.BlockSpec((tm, N), lambda i: (i, 0)),
        out_shape=jax.ShapeDtypeStruct((T, N), F32),
        compiler_params=_params("parallel"),
        name="proj_res",
    )(*lhs_list, *w_list, res)


NA_KEYS = NA_KR * GRID_W


def _natten_kernel(q_ref, k_ref, v_ref, qg_ref, kg_ref, tb_ref, o_ref, q0_ref, q1_ref, kn_ref):
    S = q_ref.shape[0]
    rows = S // GRID_W
    lane = lax.broadcasted_iota(jnp.int32, (1, LANES), 1)
    first = lane < NA_HEAD_DIM
    r_i = lax.broadcasted_iota(jnp.int32, (LANES, LANES), 0) // NA_HEAD_DIM
    c_i = lax.broadcasted_iota(jnp.int32, (LANES, LANES), 1) // NA_HEAD_DIM
    group_mean = jnp.where(r_i == c_i, 1.0 / NA_HEAD_DIM, 0.0).astype(BF16)

    def head_norm(x, g):
        xx = x * x
        hi = xx.astype(BF16)
        lo = (xx - hi.astype(F32)).astype(BF16)
        ms = (jnp.dot(hi, group_mean, preferred_element_type=F32)
              + jnp.dot(lo, group_mean, preferred_element_type=F32))
        return x * lax.rsqrt(ms + RMS_EPS) * g

    nrm_rows = 512

    def norm_body(i, carry):
        r0 = pl.multiple_of(i * nrm_rows, nrm_rows)
        qn = head_norm(q_ref[pl.ds(r0, nrm_rows), :].astype(F32), qg_ref[...]) * (NA_HEAD_DIM ** -0.5)
        q0_ref[pl.ds(r0, nrm_rows), :] = jnp.where(first, qn, 0.0).astype(BF16)
        q1_ref[pl.ds(r0, nrm_rows), :] = jnp.where(first, 0.0, qn).astype(BF16)
        kn_ref[pl.ds(r0, nrm_rows), :] = head_norm(
            k_ref[pl.ds(r0, nrm_rows), :].astype(F32), kg_ref[...]).astype(BF16)
        return carry

    lax.fori_loop(0, S // nrm_rows, norm_body, 0)

    def row_body(r, carry):
        start = jnp.clip(r - NA_KR // 2, 0, rows - NA_KR)
        dr0 = start - r + (NA_KR - 1)
        r0 = pl.multiple_of(r * GRID_W, GRID_W)
        k0 = pl.multiple_of(start * GRID_W, GRID_W)
        kk = kn_ref[pl.ds(k0, NA_KEYS), :]
        vv = v_ref[pl.ds(k0, NA_KEYS), :]
        outs = []
        for h, qh_ref in ((0, q0_ref), (1, q1_ref)):
            qh = qh_ref[pl.ds(r0, GRID_W), :]
            s = lax.dot_general(qh, kk, (((1,), (1,)), ((), ())), preferred_element_type=F32)
            s = s + tb_ref[h, dr0]
            m = jnp.max(s, axis=1, keepdims=True)
            p = jnp.exp(s - m)
            l = jnp.sum(p, axis=1, keepdims=True)
            o = jnp.dot(p.astype(BF16), vv, preferred_element_type=F32)
            outs.append(o / l)
        o_ref[pl.ds(r0, GRID_W), :] = jnp.where(first, outs[0], outs[1]).astype(o_ref.dtype)
        return carry

    lax.fori_loop(0, rows, row_body, 0)


def _natten_bias_table(rel_bias):
    qcol = np.arange(GRID_W)[:, None]
    kcol = np.arange(GRID_W)[None, :]
    dc = np.clip(kcol - qcol + (NA_KC - 1), 0, 2 * NA_KC - 2)
    win = np.clip(qcol - NA_KC // 2, 0, GRID_W - NA_KC)
    valid = (kcol >= win) & (kcol < win + NA_KC)
    dr = np.arange(NA_KR)[:, None] + np.arange(NA_KR)[None, :]
    t = rel_bias.astype(F32)[:, dr[:, :, None, None], dc[None, None, :, :]]
    t = jnp.where(valid[None, None, None], t, -jnp.inf)
    t = jnp.transpose(t, (0, 1, 3, 2, 4))
    return t.reshape(NA_HEADS, NA_KR, GRID_W, NA_KEYS)


def _natten(u3, q_gain, k_gain, rel_bias):
    B, S, _ = u3.shape
    npair = NA_HEADS // 2
    tb = _natten_bias_table(rel_bias)
    qg = jnp.tile(q_gain.astype(F32), 2).reshape(1, LANES)
    kg = jnp.tile(k_gain.astype(F32), 2).reshape(1, LANES)
    blk = lambda off: pl.BlockSpec((None, S, LANES), lambda b, p: (b, 0, off + p))
    return pl.pallas_call(
        _natten_kernel,
        grid=(B, npair),
        in_specs=[blk(0), blk(npair), blk(2 * npair),
                  pl.BlockSpec((1, LANES), lambda b, p: (0, 0)),
                  pl.BlockSpec((1, LANES), lambda b, p: (0, 0)),
                  pl.BlockSpec((2, NA_KR, GRID_W, NA_KEYS), lambda b, p: (p, 0, 0, 0))],
        out_specs=pl.BlockSpec((None, S, LANES), lambda b, p: (b, 0, p)),
        out_shape=jax.ShapeDtypeStruct((B, S, NA_WIDTH), BF16),
        scratch_shapes=[pltpu.VMEM((S, LANES), BF16)] * 3,
        compiler_params=_params("parallel", "parallel"),
        name="natten",
    )(u3, u3, u3, qg, kg, tb)


def _chan_dft_kernel(u_ref, bd_ref, o_ref):
    o_ref[...] = jnp.dot(u_ref[...], bd_ref[...], preferred_element_type=F32).astype(o_ref.dtype)


def _pos_dft_kernel(a_ref, y_ref, o_ref):
    o_ref[...] = jnp.dot(a_ref[...], y_ref[...], preferred_element_type=F32).astype(o_ref.dtype)


def _dft_tables(S):
    c = np.arange(FN_GROUP_DIM)
    ang_c = 2.0 * np.pi * ((c[:, None] * c[None, :]) % FN_GROUP_DIM) / FN_GROUP_DIM
    eye = np.eye(FN_GROUPS)
    bd = np.stack([np.kron(eye, np.cos(ang_c)), np.kron(eye, np.sin(ang_c))])
    s = jnp.arange(S, dtype=jnp.int32)
    ang_s = ((s[:, None] * s[None, :]) % S).astype(F32) * (2.0 * np.pi / S)
    norm = 1.0 / np.sqrt(S * FN_GROUP_DIM)
    a = jnp.concatenate([jnp.cos(ang_s), -jnp.sin(ang_s)], axis=1) * norm
    return jnp.asarray(bd, BF16), a.astype(BF16)


def _fourier(u3, tm=512):
    B, S, W = u3.shape
    bd, a = _dft_tables(S)
    col = (W - FN_WIDTH) // FN_WIDTH
    yy = pl.pallas_call(
        _chan_dft_kernel,
        grid=(B, 2),
        in_specs=[pl.BlockSpec((None, S, FN_WIDTH), lambda b, j: (b, 0, col)),
                  pl.BlockSpec((None, FN_WIDTH, FN_WIDTH), lambda b, j: (j, 0, 0))],
        out_specs=pl.BlockSpec((None, S, FN_WIDTH), lambda b, j: (b, j, 0)),
        out_shape=jax.ShapeDtypeStruct((B, 2 * S, FN_WIDTH), BF16),
        compiler_params=_params("parallel", "parallel"),
        name="chan_dft",
    )(u3, bd)
    return pl.pallas_call(
        _pos_dft_kernel,
        grid=(S // tm, B),
        in_specs=[pl.BlockSpec((tm, 2 * S), lambda i, b: (i, 0)),
                  pl.BlockSpec((None, 2 * S, FN_WIDTH), lambda i, b: (b, 0, 0))],
        out_specs=pl.BlockSpec((None, tm, FN_WIDTH), lambda i, b: (b, i, 0)),
        out_shape=jax.ShapeDtypeStruct((B, S, FN_WIDTH), BF16),
        compiler_params=_params("parallel", "parallel"),
        name="pos_dft",
    )(a, yy)


def _mlstm_kernel(q_ref, k_ref, v_ref, og_ref, g_ref, hg_ref, o_ref, hacc_ref, bf_ref, bb_ref):
    L = ML_CHUNK
    nc = q_ref.shape[0] // L
    head = pl.program_id(1)
    lane = lax.broadcasted_iota(jnp.int32, (1, LANES), 1)
    lo = (head % 2) * ML_QK_DIM
    qmask = (lane >= lo) & (lane < lo + ML_QK_DIM)
    row = lax.broadcasted_iota(jnp.int32, (L, L), 0)
    col = lax.broadcasted_iota(jnp.int32, (L, L), 1)
    eye = row == col

    bf_ref[...] = _dot_exact_rhs(jax.nn.log_sigmoid(g_ref[1]), (row <= col).astype(BF16))
    bb_ref[...] = _dot_exact_rhs(jax.nn.log_sigmoid(g_ref[3]), (row >= col).astype(BF16))

    def to_col(r):
        return jnp.sum(jnp.where(eye, r, 0.0), axis=1, keepdims=True)

    def sweep(fw):
        b_ref = bf_ref if fw else bb_ref
        gi = 0 if fw else 2
        causal = (col <= row) if fw else (col >= row)
        last = L - 1 if fw else 0

        def body(t, carry):
            C, n, m = carry
            c = t if fw else nc - 1 - t
            r0 = pl.multiple_of(c * L, L)
            brow = b_ref[pl.ds(c, 1), :]
            irow = g_ref[gi, pl.ds(c, 1), :]
            g = brow[:, last:last + 1]
            m_loc = jnp.max(g - brow + irow, axis=1, keepdims=True)
            rb = jnp.broadcast_to(brow, (L, L))
            ri = jnp.broadcast_to(irow, (L, L))
            bcol = to_col(rb)
            icol = to_col(ri)
            wcol = jnp.exp(g - bcol + icol - m_loc)

            qm = jnp.where(qmask, q_ref[pl.ds(r0, L), :], jnp.zeros((), BF16))
            ks = k_ref[pl.ds(r0, L), :] * jnp.asarray(ML_QK_DIM ** -0.5, BF16)
            vc = v_ref[pl.ds(r0, L), :]
            kw = ks.astype(F32) * wcol
            dn = jnp.sum(kw, axis=0, keepdims=True)
            dC = lax.dot_general(kw.astype(BF16), vc, (((0,), (0,)), ((), ())),
                                 preferred_element_type=F32)

            d_log = jnp.where(causal, bcol - rb + ri, -jnp.inf)
            inter_log = bcol + m
            m_out = jnp.maximum(inter_log, jnp.max(d_log, axis=1, keepdims=True))
            d_w = jnp.exp(d_log - m_out)
            inter_w = jnp.exp(inter_log - m_out)
            s = lax.dot_general(qm, ks, (((1,), (1,)), ((), ())), preferred_element_type=F32) * d_w
            num = (jnp.dot(s.astype(BF16), vc, preferred_element_type=F32)
                   + inter_w * jnp.dot(qm, C.astype(BF16), preferred_element_type=F32))
            den = (jnp.sum(s, axis=1, keepdims=True)
                   + inter_w * jnp.sum(qm.astype(F32) * n, axis=1, keepdims=True))
            hh = num / jnp.maximum(jnp.abs(den), jnp.exp(-m_out))

            if fw:
                hacc_ref[pl.ds(r0, L), :] = hh
            else:
                ht = hacc_ref[pl.ds(r0, L), :] + hh
                y = ht * lax.rsqrt(jnp.mean(ht * ht, axis=-1, keepdims=True) + RMS_EPS) * hg_ref[...]
                y = y * jax.nn.sigmoid(og_ref[pl.ds(r0, L), :].astype(F32))
                o_ref[pl.ds(r0, L), :] = y.astype(o_ref.dtype)

            m_new = jnp.maximum(g + m, m_loc)
            a = jnp.exp(g + m - m_new)
            cc = jnp.exp(m_loc - m_new)
            return a * C + cc * dC, a * n + cc * dn, m_new

        init = (jnp.zeros((LANES, ML_V_DIM), F32), jnp.zeros((1, LANES), F32), jnp.zeros((1, 1), F32))
        lax.fori_loop(0, nc, body, init)

    sweep(True)
    sweep(False)


def _mlstm(u3, gates, h_gain):
    B, S, _ = u3.shape
    nc = S // ML_CHUNK
    qk_blocks = ML_QK_WIDTH // LANES
    v_off = 2 * qk_blocks
    o_off = v_off + ML_HEADS
    hg = h_gain.astype(F32).reshape(ML_HEADS, 1, ML_V_DIM)
    return pl.pallas_call(
        _mlstm_kernel,
        grid=(B, ML_HEADS),
        in_specs=[pl.BlockSpec((None, S, LANES), lambda b, h: (b, 0, h // 2)),
                  pl.BlockSpec((None, S, LANES), lambda b, h: (b, 0, qk_blocks + h // 2)),
                  pl.BlockSpec((None, S, LANES), lambda b, h: (b, 0, v_off + h)),
                  pl.BlockSpec((None, S, LANES), lambda b, h: (b, 0, o_off + h)),
                  pl.BlockSpec((4, None, None, nc, ML_CHUNK), lambda b, h: (0, b, h, 0, 0)),
                  pl.BlockSpec((None, 1, ML_V_DIM), lambda b, h: (h, 0, 0))],
        out_specs=pl.BlockSpec((None, S, LANES), lambda b, h: (b, 0, h)),
        out_shape=jax.ShapeDtypeStruct((B, S, ML_WIDTH), BF16),
        scratch_shapes=[pltpu.VMEM((S, ML_V_DIM), F32),
                        pltpu.VMEM((nc, ML_CHUNK), F32),
                        pltpu.VMEM((nc, ML_CHUNK), F32)],
        compiler_params=_params("parallel", "parallel"),
        name="mlstm",
    )(u3, u3, u3, u3, gates, hg)


def _moe_ffn_kernel(be_ref, nv_ref, x_ref, wg_ref, wu_ref, wd_ref, o_ref):
    i = pl.program_id(0)

    @pl.when(i < nv_ref[0])
    def _():
        x = x_ref[...]
        hg = jnp.dot(x, wg_ref[...], preferred_element_type=F32)
        hu = jnp.dot(x, wu_ref[...], preferred_element_type=F32)
        h = (hg * jax.nn.sigmoid(hg) * hu).astype(BF16)
        o_ref[...] = jnp.dot(h, wd_ref[...], preferred_element_type=F32).astype(o_ref.dtype)

    @pl.when(i >= nv_ref[0])
    def _():
        o_ref[...] = jnp.zeros_like(o_ref)


def _moe_ffn(xg, block_e, n_valid, w_gate, w_up, w_down):
    P, D = xg.shape
    n_blocks = P // MOE_BLOCK
    grid_spec = pltpu.PrefetchScalarGridSpec(
        num_scalar_prefetch=2, grid=(n_blocks,),
        in_specs=[pl.BlockSpec((MOE_BLOCK, D), lambda i, be, nv: (i, 0)),
                  pl.BlockSpec((None, D, MOE_FF), lambda i, be, nv: (be[i], 0, 0)),
                  pl.BlockSpec((None, D, MOE_FF), lambda i, be, nv: (be[i], 0, 0)),
                  pl.BlockSpec((None, MOE_FF, D), lambda i, be, nv: (be[i], 0, 0))],
        out_specs=pl.BlockSpec((MOE_BLOCK, D), lambda i, be, nv: (i, 0)))
    return pl.pallas_call(
        _moe_ffn_kernel, grid_spec=grid_spec,
        out_shape=jax.ShapeDtypeStruct((P, D), BF16),
        compiler_params=_params("arbitrary"),
        name="moe_ffn",
    )(block_e, n_valid, xg, w_gate, w_up, w_down)


def _route(logits, b_rg, b_re):
    T = logits.shape[0]
    pg = jax.nn.softmax(logits[:, :MOE_GROUPS] + b_rg.astype(F32), axis=-1)
    p_top, g_sel = lax.top_k(pg, 1)
    le = (logits[:, MOE_GROUPS:MOE_GROUPS + MOE_EXPERTS] + b_re.astype(F32)).reshape(
        T, MOE_GROUPS, MOE_EXPERTS_PER_GROUP)
    le_sel = jnp.take_along_axis(le, g_sel[:, :, None], axis=1)[:, 0]
    pe = jax.nn.softmax(le_sel, axis=-1)
    v2, i2 = lax.top_k(pe, MOE_TOP_K)
    gate = p_top * v2 / jnp.sum(v2, axis=-1, keepdims=True)
    eid = g_sel * MOE_EXPERTS_PER_GROUP + i2
    return eid.astype(jnp.int32), gate


def _dispatch_plan(eid, T):
    A = T * MOE_TOP_K
    n_blocks = (A + MOE_EXPERTS * (MOE_BLOCK - 1) + MOE_BLOCK - 1) // MOE_BLOCK
    P = n_blocks * MOE_BLOCK
    flat_e = eid.reshape(-1)
    onehot = (flat_e[:, None] == jnp.arange(MOE_EXPERTS, dtype=jnp.int32)[None, :]).astype(jnp.int32)
    csum = jnp.cumsum(onehot, axis=0)
    rank = jnp.take_along_axis(csum, flat_e[:, None], axis=1)[:, 0] - 1
    counts = csum[-1]
    padded = (counts + MOE_BLOCK - 1) // MOE_BLOCK * MOE_BLOCK
    pend = jnp.cumsum(padded)
    pstarts = pend - padded
    pos = pstarts[flat_e] + rank
    flat_tok = jnp.arange(A, dtype=jnp.int32) // MOE_TOP_K
    row_tok = jnp.full((P,), T, jnp.int32).at[pos].set(flat_tok)
    block_start = jnp.arange(n_blocks, dtype=jnp.int32) * MOE_BLOCK
    block_e = jnp.minimum(jnp.searchsorted(pend, block_start, side='right'), MOE_EXPERTS - 1)
    n_valid = (pend[-1] // MOE_BLOCK).reshape(1)
    return row_tok, pos.reshape(T, MOE_TOP_K), block_e.astype(jnp.int32), n_valid.astype(jnp.int32)


def _hier_moe(x2, norm_gain, w_rg, b_rg, w_re, b_re, w_gate, w_up, w_down):
    T, D = x2.shape
    w_r = jnp.zeros((D, LANES), F32).at[:, :MOE_GROUPS].set(w_rg.astype(F32))
    w_r = w_r.at[:, MOE_GROUPS:MOE_GROUPS + MOE_EXPERTS].set(w_re.astype(F32))
    logits, xn = _norm_proj(x2, norm_gain, w_aux=w_r, b_aux=jnp.zeros((1, LANES), F32), want_xn=True)
    eid, gate = _route(logits, b_rg, b_re)
    row_tok, pos, block_e, n_valid = _dispatch_plan(eid, T)
    xn_pad = jnp.concatenate([xn, jnp.zeros((1, D), xn.dtype)], axis=0)
    xg = jnp.take(xn_pad, row_tok, axis=0)
    y_rows = _moe_ffn(xg, block_e, n_valid, w_gate.astype(BF16), w_up.astype(BF16), w_down.astype(BF16))
    y = (jnp.take(y_rows, pos[:, 0], axis=0).astype(F32) * gate[:, 0:1]
         + jnp.take(y_rows, pos[:, 1], axis=0).astype(F32) * gate[:, 1:2])
    return x2 + y


def _even_layer(x2, B, S, norm_gain, w_in, q_gain, k_gain, rel_bias, w_out):
    (u,) = _norm_proj(x2, norm_gain, w_main=w_in.astype(BF16))
    u3 = u.reshape(B, S, u.shape[1])
    a = _natten(u3, q_gain, k_gain, rel_bias).reshape(B * S, NA_WIDTH)
    z = _fourier(u3).reshape(B * S, FN_WIDTH)
    w_out = w_out.astype(BF16)
    return _proj_res([a, z], [w_out[:NA_WIDTH], w_out[NA_WIDTH:]], x2)


def _odd_layer(x2, B, S, norm_gain, w_in, gate_bias, h_gain, w_out):
    n_main = 2 * ML_QK_WIDTH + 2 * ML_WIDTH
    n_gate = 4 * ML_HEADS
    w_g = jnp.zeros((D_MODEL, LANES), F32).at[:, :n_gate].set(w_in[:, n_main:].astype(F32))
    b_g = jnp.zeros((1, LANES), F32).at[0, :n_gate].set(gate_bias.astype(F32))
    u, gates = _norm_proj(x2, norm_gain, w_main=w_in[:, :n_main].astype(BF16), w_aux=w_g, b_aux=b_g)
    nc = S // ML_CHUNK
    gates = gates[:, :n_gate].reshape(B, nc, ML_CHUNK, 4, ML_HEADS).transpose(3, 0, 4, 1, 2)
    hgated = _mlstm(u.reshape(B, S, n_main), gates, h_gain)
    return _proj_res([hgated.reshape(B * S, ML_WIDTH)], [w_out.astype(BF16)], x2)


def kernel(x, ev_norm, ev_w_in, ev_q_gain, ev_k_gain, ev_rel_bias, ev_w_out, od_norm, od_w_in, od_gate_bias, od_h_gain, od_w_out, moe_norm, moe_w_rg, moe_b_rg, moe_w_re, moe_b_re, moe_w_gate, moe_w_up, moe_w_down):
    B, S, D = x.shape
    depth = moe_norm.shape[0]
    x2 = x.reshape(B * S, D)
    for layer in range(depth):
        i = layer // 2
        if layer % 2 == 0:
            x2 = _even_layer(x2, B, S, ev_norm[i], ev_w_in[i], ev_q_gain[i], ev_k_gain[i],
                             ev_rel_bias[i], ev_w_out[i])
        else:
            x2 = _odd_layer(x2, B, S, od_norm[i], od_w_in[i], od_gate_bias[i], od_h_gain[i], od_w_out[i])
        x2 = _hier_moe(x2, moe_norm[layer], moe_w_rg[layer], moe_b_rg[layer], moe_w_re[layer],
                       moe_b_re[layer], moe_w_gate[layer], moe_w_up[layer], moe_w_down[layer])
    return x2.reshape(B, S, D)
```
